```python
import jax, jax.numpy as jnp
from jax import lax
import numpy as np

D_MODEL = 1024
BATCH = 4
SEQ = 4096
DEPTH = 2

N_MIXERS = 2
N_ATTN_LAYERS = (DEPTH + 1) // 2
N_RWKV_LAYERS = DEPTH // 2

HEAD_DIM = 64
ATTN_Q_HEADS = D_MODEL // HEAD_DIM
ATTN_KV_HEADS = 4
ATTN_GROUP = ATTN_Q_HEADS // ATTN_KV_HEADS
QKV_WIDTH = (ATTN_Q_HEADS + 2 * ATTN_KV_HEADS) * HEAD_DIM
WINDOW = 128
BLOCK = WINDOW
ROPE_THETA = 10000.0

RWKV_HEAD = 64
RWKV_HEADS = D_MODEL // RWKV_HEAD
DECAY_LORA = 64
ICLR_LORA = 64
GATE_LORA = 160
LNX_EPS = 64e-5

D_FF = 2816
RMS_EPS = 1e-6

kernel_name = "hybrid_swa_sink_rwkv7_macaron"


def rms_norm(x, g):
    xf = x.astype(jnp.float32)
    y = xf * lax.rsqrt(jnp.mean(xf * xf, axis=-1, keepdims=True) + RMS_EPS) * g
    return y.astype(x.dtype)


def swiglu(h, w_in, w_out):
    gu = h @ w_in
    gate, up = gu[..., :D_FF], gu[..., D_FF:]
    return (jax.nn.silu(gate) * up) @ w_out


def rope_tables(seq):
    inv_freq = ROPE_THETA ** (-jnp.arange(0, HEAD_DIM, 2, dtype=jnp.float32) / HEAD_DIM)
    ang = jnp.arange(seq, dtype=jnp.float32)[:, None] * inv_freq[None, :]
    return jnp.cos(ang)[:, None, :], jnp.sin(ang)[:, None, :]


def apply_rope(x, cos, sin):
    xf = x.astype(jnp.float32)
    x1, x2 = xf[..., :HEAD_DIM // 2], xf[..., HEAD_DIM // 2:]
    out = jnp.concatenate([x1 * cos - x2 * sin, x2 * cos + x1 * sin], axis=-1)
    return out.astype(x.dtype)


def swa_sink_attention(h, w_qkv, b_qkv, sinks, w_o):
    B, S, _ = h.shape
    nb = S // BLOCK
    qkv = h @ w_qkv + b_qkv
    nq = ATTN_Q_HEADS * HEAD_DIM
    nk = ATTN_KV_HEADS * HEAD_DIM
    q = qkv[..., :nq].reshape(B, S, ATTN_Q_HEADS, HEAD_DIM)
    k = qkv[..., nq:nq + nk].reshape(B, S, ATTN_KV_HEADS, HEAD_DIM)
    v = qkv[..., nq + nk:].reshape(B, S, ATTN_KV_HEADS, HEAD_DIM)
    cos, sin = rope_tables(S)
    q = apply_rope(q, cos, sin)
    k = apply_rope(k, cos, sin)

    qb = q.reshape(B, nb, BLOCK, ATTN_KV_HEADS, ATTN_GROUP, HEAD_DIM)
    pad = ((0, 0), (BLOCK, 0), (0, 0), (0, 0))
    kb = jnp.pad(k, pad).reshape(B, nb + 1, BLOCK, ATTN_KV_HEADS, HEAD_DIM)
    vb = jnp.pad(v, pad).reshape(B, nb + 1, BLOCK, ATTN_KV_HEADS, HEAD_DIM)
    k_band = jnp.concatenate([kb[:, :-1], kb[:, 1:]], axis=2)
    v_band = jnp.concatenate([vb[:, :-1], vb[:, 1:]], axis=2)

    scale = HEAD_DIM ** -0.5
    s = jnp.einsum('bnqhgd,bnkhd->bnhgqk', qb, k_band).astype(jnp.float32) * scale
    qi = jnp.arange(BLOCK)[:, None]
    kj = jnp.arange(2 * BLOCK)[None, :]
    diff = qi - kj + BLOCK
    kpos = jnp.arange(nb)[:, None, None] * BLOCK - BLOCK + kj[None]
    mask = (diff >= 0) & (diff < WINDOW) & (kpos >= 0)
    mask = mask[None, :, None, None]
    s = jnp.where(mask, s, -jnp.inf)
    sink = sinks.astype(jnp.float32).reshape(ATTN_KV_HEADS, ATTN_GROUP)[None, None, :, :, None, None]
    m = jnp.maximum(jnp.max(s, axis=-1, keepdims=True), sink)
    p = jnp.exp(s - m)
    denom = jnp.sum(p, axis=-1, keepdims=True) + jnp.exp(sink - m)
    p = p / denom
    o = jnp.einsum('bnhgqk,bnkhd->bnqhgd', p, v_band.astype(jnp.float32))
    o = o.reshape(B, S, ATTN_Q_HEADS * HEAD_DIM).astype(h.dtype)
    return o @ w_o


def wkv7_scan(r, w, k, v, a, b):
    B, S, H, N = r.shape

    def step(state, inp):
        r_t, w_t, k_t, v_t, a_t, b_t = inp
        sa = jnp.einsum('bhij,bhj->bhi', state, a_t)
        state = (state * w_t[:, :, None, :]
                 + sa[..., None] * b_t[:, :, None, :]
                 + v_t[..., None] * k_t[:, :, None, :])
        y = jnp.einsum('bhij,bhj->bhi', state, r_t)
        return state, y

    xs = tuple(jnp.moveaxis(t, 1, 0) for t in (r, w, k, v, a, b))
    s0 = jnp.zeros((B, H, N, N), jnp.float32)
    _, ys = lax.scan(step, s0, xs)
    return jnp.moveaxis(ys, 0, 1)


def rwkv7_time_mix(h, mix, w_rkv, w_o, w0, w1, w2, a0, a1, a2, g1, g2,
                   k_k, k_a, r_k, lnx_g, lnx_b):
    B, S, D = h.shape
    H, N = RWKV_HEADS, RWKV_HEAD
    xx = jnp.pad(h, ((0, 0), (1, 0), (0, 0)))[:, :-1] - h
    xr = h + xx * mix[0]
    xw = h + xx * mix[1]
    xk = h + xx * mix[2]
    xv = h + xx * mix[3]
    xa = h + xx * mix[4]
    xg = h + xx * mix[5]

    r = xr @ w_rkv[0]
    k = xk @ w_rkv[1]
    v = xv @ w_rkv[2]
    logw = -jax.nn.softplus(-(w0 + jnp.tanh(xw @ w1) @ w2)) - 0.5
    decay = jnp.exp(-jnp.exp(logw.astype(jnp.float32)))
    iclr = jax.nn.sigmoid(a0 + (xa @ a1) @ a2)
    g = jax.nn.sigmoid(xg @ g1) @ g2

    heads = lambda t: t.reshape(B, S, H, N).astype(jnp.float32)
    kk = heads(k * k_k)
    kk = kk / jnp.maximum(jnp.sqrt(jnp.sum(kk * kk, axis=-1, keepdims=True)), 1e-12)
    k = k * (1.0 + (iclr - 1.0) * k_a)

    rh, kh, vh, dh, ih = heads(r), heads(k), heads(v), heads(decay), heads(iclr)
    y = wkv7_scan(rh, dh, kh, vh, -kk, kk * ih)

    mu = jnp.mean(y, axis=-1, keepdims=True)
    var = jnp.mean((y - mu) ** 2, axis=-1, keepdims=True)
    yn = ((y - mu) * lax.rsqrt(var + LNX_EPS)).reshape(B, S, D) * lnx_g + lnx_b
    bonus = (jnp.sum(rh * kh * r_k, axis=-1, keepdims=True) * vh).reshape(B, S, D)
    out = (yn + bonus).astype(h.dtype) * g
    return out @ w_o


def setup_inputs(seed: int = 0) -> dict:
    key = jax.random.key(seed)
    ks = jax.random.split(key, 32)
    f32 = jnp.float32
    nrm = lambda k, shape, s: jax.random.normal(k, shape, f32) * s
    NA, NR, D = N_ATTN_LAYERS, N_RWKV_LAYERS, D_MODEL
    return {
        "x": nrm(ks[0], (BATCH, SEQ, D), 1.0),
        "norm_g": 1.0 + nrm(ks[1], (DEPTH, 6, D), 0.05),
        "ffn_w_in": nrm(ks[2], (DEPTH, 2, D, 2 * D_FF), D ** -0.5),
        "ffn_w_out": nrm(ks[3], (DEPTH, 2, D_FF, D), D_FF ** -0.5),
        "attn_w_qkv": nrm(ks[4], (NA, D, QKV_WIDTH), D ** -0.5),
        "attn_b_qkv": nrm(ks[5], (NA, QKV_WIDTH), 0.02),
        "attn_sinks": nrm(ks[6], (NA, ATTN_Q_HEADS), 1.0),
        "attn_w_o": nrm(ks[7], (NA, ATTN_Q_HEADS * HEAD_DIM, D), (ATTN_Q_HEADS * HEAD_DIM) ** -0.5),
        "rwkv_mix": jax.random.uniform(ks[8], (NR, 6, D), f32),
        "rwkv_w_rkv": nrm(ks[9], (NR, 3, D, D), D ** -0.5),
        "rwkv_w_o": nrm(ks[10], (NR, D, D), D ** -0.5),
        "rwkv_w0": jax.random.uniform(ks[11], (NR, D), f32, minval=-6.0, maxval=-0.5),
        "rwkv_w1": nrm(ks[12], (NR, D, DECAY_LORA), D ** -0.5),
        "rwkv_w2": nrm(ks[13], (NR, DECAY_LORA, D), 0.1 * DECAY_LORA ** -0.5),
        "rwkv_a0": nrm(ks[14], (NR, D), 0.1),
        "rwkv_a1": nrm(ks[15], (NR, D, ICLR_LORA), D ** -0.5),
        "rwkv_a2": nrm(ks[16], (NR, ICLR_LORA, D), 0.3 * ICLR_LORA ** -0.5),
        "rwkv_g1": nrm(ks[17], (NR, D, GATE_LORA), D ** -0.5),
        "rwkv_g2": nrm(ks[18], (NR, GATE_LORA, D), GATE_LORA ** -0.5),
        "rwkv_k_k": 0.85 + nrm(ks[19], (NR, D), 0.05),
        "rwkv_k_a": 1.0 + nrm(ks[20], (NR, D), 0.05),
        "rwkv_r_k": nrm(ks[21], (NR, RWKV_HEADS, RWKV_HEAD), 0.1),
        "rwkv_lnx_g": 1.0 + nrm(ks[22], (NR, D), 0.05),
        "rwkv_lnx_b": nrm(ks[23], (NR, D), 0.02),
    }


def reference(x, norm_g, ffn_w_in, ffn_w_out, attn_w_qkv, attn_b_qkv, attn_sinks, attn_w_o,
              rwkv_mix, rwkv_w_rkv, rwkv_w_o, rwkv_w0, rwkv_w1, rwkv_w2, rwkv_a0, rwkv_a1,
              rwkv_a2, rwkv_g1, rwkv_g2, rwkv_k_k, rwkv_k_a, rwkv_r_k, rwkv_lnx_g, rwkv_lnx_b):
    for i in range(DEPTH):
        g = norm_g[i]
        x = x + 0.5 * rms_norm(swiglu(rms_norm(x, g[0]), ffn_w_in[i, 0], ffn_w_out[i, 0]), g[1])
        h = rms_norm(x, g[2])
        j = i // N_MIXERS
        if i % N_MIXERS == 0:
            m = swa_sink_attention(h, attn_w_qkv[j], attn_b_qkv[j], attn_sinks[j], attn_w_o[j])
        else:
            m = rwkv7_time_mix(h, rwkv_mix[j], rwkv_w_rkv[j], rwkv_w_o[j], rwkv_w0[j], rwkv_w1[j],
                               rwkv_w2[j], rwkv_a0[j], rwkv_a1[j], rwkv_a2[j], rwkv_g1[j],
                               rwkv_g2[j], rwkv_k_k[j], rwkv_k_a[j], rwkv_r_k[j],
                               rwkv_lnx_g[j], rwkv_lnx_b[j])
        x = x + rms_norm(m, g[3])
        x = x + 0.5 * rms_norm(swiglu(rms_norm(x, g[4]), ffn_w_in[i, 1], ffn_w_out[i, 1]), g[5])
    return x
```

```python
import functools

import jax
import jax.numpy as jnp
from jax import lax
from jax.experimental import pallas as pl
from jax.experimental.pallas import tpu as pltpu

D_MODEL = 1024
D_FF = 2816
RMS_EPS = 1e-6

HEAD_DIM = 64
Q_HEADS = 16
KV_HEADS = 4
KV_WIDTH = KV_HEADS * HEAD_DIM
QKV_WIDTH = D_MODEL + 2 * KV_WIDTH
WINDOW = 128
ROPE_THETA = 10000.0

RWKV_HEAD = 64
LNX_EPS = 64e-5
GATE_LORA_PADDED = 256

LANES = 128
PAIR = 2 * RWKV_HEAD
N_PAIRS = D_MODEL // PAIR
CHUNK = 64

VMEM_LIMIT_BYTES = 56 * 1024 * 1024

BF16 = jnp.bfloat16
F32 = jnp.float32


def _dot(a, b):
    return jnp.dot(a, b, preferred_element_type=F32)


def _dot_nt(a, b):
    return lax.dot_general(a, b, (((1,), (1,)), ((), ())), preferred_element_type=F32)


def _rms(x, g):
    return x * lax.rsqrt(jnp.mean(x * x, axis=-1, keepdims=True) + RMS_EPS) * g


def _params(semantics):
    return pltpu.CompilerParams(dimension_semantics=semantics, vmem_limit_bytes=VMEM_LIMIT_BYTES)


def _ffn_kernel(x_ref, gpre_ref, gpost_ref, wg_ref, wu_ref, wo_ref, o_ref, h_scr, acc_scr):
    j = pl.program_id(1)

    @pl.when(j == 0)
    def _():
        h_scr[...] = _rms(x_ref[...], gpre_ref[...]).astype(BF16)

    h = h_scr[...]
    gate = _dot(h, wg_ref[...])
    up = _dot(h, wu_ref[...])
    act = (gate * jax.nn.sigmoid(gate) * up).astype(BF16)
    part = _dot(act, wo_ref[...])

    @pl.when(j == 0)
    def _():
        acc_scr[...] = part

    @pl.when(j > 0)
    def _():
        acc_scr[...] += part

    @pl.when(j == pl.num_programs(1) - 1)
    def _():
        o_ref[...] = x_ref[...] + 0.5 * _rms(acc_scr[...], gpost_ref[...])


def _ffn(x, g_pre, g_post, w_in, w_out, *, tm=512, tf=1408):
    n, d = x.shape
    nj = D_FF // tf
    w_in = w_in.astype(BF16)
    w_out = w_out.astype(BF16)
    return pl.pallas_call(
        _ffn_kernel,
        grid=(n // tm, nj),
        in_specs=[
            pl.BlockSpec((tm, d), lambda i, j: (i, 0)),
            pl.BlockSpec((1, d), lambda i, j: (0, 0)),
            pl.BlockSpec((1, d), lambda i, j: (0, 0)),
            pl.BlockSpec((d, tf), lambda i, j: (0, j)),
            pl.BlockSpec((d, tf), lambda i, j: (0, nj + j)),
            pl.BlockSpec((tf, d), lambda i, j: (j, 0)),
        ],
        out_specs=pl.BlockSpec((tm, d), lambda i, j: (i, 0)),
        out_shape=jax.ShapeDtypeStruct((n, d), F32),
        scratch_shapes=[pltpu.VMEM((tm, d), BF16), pltpu.VMEM((tm, d), F32)],
        compiler_params=_params(("parallel", "arbitrary")),
        name="ffn",
    )(x, g_pre.reshape(1, d), g_post.reshape(1, d), w_in, w_in, w_out)


def _rope(t, cos, sin_lo, sin_hi):
    width = t.shape[-1]
    reps = width // LANES
    cos = jnp.tile(cos, (1, reps))
    sin_lo = jnp.tile(sin_lo, (1, reps))
    sin_hi = jnp.tile(sin_hi, (1, reps))
    half = HEAD_DIM // 2
    from_above = pltpu.roll(t, width - half, 1)
    from_below = pltpu.roll(t, half, 1)
    return t * cos + from_above * sin_lo + from_below * sin_hi


def _qkv_kernel(x_ref, g_ref, w_ref, b_ref, cos_ref, slo_ref, shi_ref, q_ref, k_ref, v_ref):
    h = _rms(x_ref[...], g_ref[...]).astype(BF16)
    qkv = _dot(h, w_ref[...]) + b_ref[...]
    cos, slo, shi = cos_ref[...], slo_ref[...], shi_ref[...]
    q = _rope(qkv[:, :D_MODEL], cos, slo, shi) * (HEAD_DIM ** -0.5)
    k = _rope(qkv[:, D_MODEL:D_MODEL + KV_WIDTH], cos, slo, shi)
    q_ref[...] = q.astype(BF16)
    k_ref[...] = k.astype(BF16)
    v_ref[...] = qkv[:, D_MODEL + KV_WIDTH:].astype(BF16)


def _rope_tables(seq):
    half = HEAD_DIM // 2
    inv_freq = ROPE_THETA ** (-jnp.arange(0, HEAD_DIM, 2, dtype=F32) / HEAD_DIM)
    ang = jnp.arange(seq, dtype=F32)[:, None] * inv_freq[None, :]
    cos, sin = jnp.cos(ang), jnp.sin(ang)
    zero = jnp.zeros_like(sin)
    reps = LANES // HEAD_DIM
    cos_t = jnp.tile(jnp.concatenate([cos, cos], -1), (1, reps))
    sin_lo = jnp.tile(jnp.concatenate([-sin, zero], -1), (1, reps))
    sin_hi = jnp.tile(jnp.concatenate([zero, sin], -1), (1, reps))
    del half
    return cos_t, sin_lo, sin_hi


def _qkv(x, g, w_qkv, b_qkv, seq, *, tm=512):
    n, d = x.shape
    tm = min(tm, seq)
    cos_t, sin_lo, sin_hi = _rope_tables(seq)
    per_seq = seq // tm
    row = lambda i: (i, 0)
    fixed = lambda i: (0, 0)
    table = lambda i: (i % per_seq, 0)
    return pl.pallas_call(
        _qkv_kernel,
        grid=(n // tm,),
        in_specs=[
            pl.BlockSpec((tm, d), row),
            pl.BlockSpec((1, d), fixed),
            pl.BlockSpec((d, QKV_WIDTH), fixed),
            pl.BlockSpec((1, QKV_WIDTH), fixed),
            pl.BlockSpec((tm, LANES), table),
            pl.BlockSpec((tm, LANES), table),
            pl.BlockSpec((tm, LANES), table),
        ],
        out_specs=[
            pl.BlockSpec((tm, D_MODEL), row),
            pl.BlockSpec((tm, KV_WIDTH), row),
            pl.BlockSpec((tm, KV_WIDTH), row),
        ],
        out_shape=[
            jax.ShapeDtypeStruct((n, D_MODEL), BF16),
            jax.ShapeDtypeStruct((n, KV_WIDTH), BF16),
            jax.ShapeDtypeStruct((n, KV_WIDTH), BF16),
        ],
        compiler_params=_params(("parallel",)),
        name="attn_qkv",
    )(x, g.reshape(1, d), w_qkv.astype(BF16), b_qkv.reshape(1, QKV_WIDTH), cos_t, sin_lo, sin_hi)


def _attn_kernel(x_ref, q_ref, kp_ref, kc_ref, vp_ref, vc_ref, sink_ref, wo_ref, g_ref, o_ref):
    n = pl.program_id(1)
    qi = lax.broadcasted_iota(jnp.int32, (WINDOW, 2 * WINDOW), 0)
    kj = lax.broadcasted_iota(jnp.int32, (WINDOW, 2 * WINDOW), 1)
    in_prev = (kj < WINDOW) & (kj > qi) & (n > 0)
    in_own = (kj >= WINDOW) & (kj - WINDOW <= qi)
    mask = in_prev | in_own
    group = Q_HEADS // KV_HEADS
    outs = []
    for h in range(Q_HEADS):
        g = h // group
        kv_cols = slice(g * HEAD_DIM, (g + 1) * HEAD_DIM)
        q_h = q_ref[:, h * HEAD_DIM:(h + 1) * HEAD_DIM]
        k_band = jnp.concatenate([kp_ref[:, kv_cols], kc_ref[:, kv_cols]], axis=0)
        v_band = jnp.concatenate([vp_ref[:, kv_cols], vc_ref[:, kv_cols]], axis=0)
        s = jnp.where(mask, _dot_nt(q_h, k_band), -jnp.inf)
        sink = sink_ref[h:h + 1, :][:, :1]
        m = jnp.maximum(jnp.max(s, axis=-1, keepdims=True), sink)
        p = jnp.exp(s - m)
        denom = jnp.sum(p, axis=-1, keepdims=True) + jnp.exp(sink - m)
        outs.append(_dot(p.astype(BF16), v_band) / denom)
    o = jnp.concatenate(outs, axis=-1).astype(BF16)
    mixed = _dot(o, wo_ref[...])
    o_ref[...] = x_ref[...] + _rms(mixed, g_ref[...])


def _attention(x, q, k, v, sinks, w_o, g_post, batch, seq):
    n, d = x.shape
    nb = seq // WINDOW
    cur = lambda b, j: (b * nb + j, 0)
    prev = lambda b, j: (b * nb + jnp.maximum(j - 1, 0), 0)
    fixed = lambda b, j: (0, 0)
    sink_rows = jnp.broadcast_to(sinks.astype(F32)[:, None], (Q_HEADS, LANES))
    return pl.pallas_call(
        _attn_kernel,
        grid=(batch, nb),
        in_specs=[
            pl.BlockSpec((WINDOW, d), cur),
            pl.BlockSpec((WINDOW, D_MODEL), cur),
            pl.BlockSpec((WINDOW, KV_WIDTH), prev),
            pl.BlockSpec((WINDOW, KV_WIDTH), cur),
            pl.BlockSpec((WINDOW, KV_WIDTH), prev),
            pl.BlockSpec((WINDOW, KV_WIDTH), cur),
            pl.BlockSpec((Q_HEADS, LANES), fixed),
            pl.BlockSpec((D_MODEL, d), fixed),
            pl.BlockSpec((1, d), fixed),
        ],
        out_specs=pl.BlockSpec((WINDOW, d), cur),
        out_shape=jax.ShapeDtypeStruct((n, d), F32),
        compiler_params=_params(("parallel", "parallel")),
        name="attn_core",
    )(x, q, k, k, v, v, sink_rows, w_o.astype(BF16), g_post.reshape(1, d))


def _split_hi_lo(x):
    hi = x.astype(BF16)
    lo = (x - hi.astype(F32)).astype(BF16)
    return hi, lo


def _head_sums(x):
    blk = 2 * LANES
    r = lax.broadcasted_iota(jnp.int32, (blk, blk), 0) // RWKV_HEAD
    c = lax.broadcasted_iota(jnp.int32, (blk, blk), 1) // RWKV_HEAD
    ones = jnp.where(r == c, 1.0, 0.0).astype(BF16)
    hi, lo = _split_hi_lo(x)
    cols = []
    for j in range(x.shape[-1] // blk):
        sl = slice(j * blk, (j + 1) * blk)
        cols.append(_dot(hi[:, sl], ones) + _dot(lo[:, sl], ones))
    return jnp.concatenate(cols, axis=-1)


def _softplus(u):
    return jnp.maximum(u, 0.0) + jnp.log(1.0 + jnp.exp(-jnp.abs(u)))


def _rwkv_pre_kernel(x_ref, xp_ref, g_ref, mix_ref, wr_ref, wk_ref, wv_ref, w0_ref, w1_ref, w2_ref,
                     a0_ref, a1_ref, a2_ref, g1_ref, g2_ref, kk_ref, ka_ref,
                     r_out, lw_out, k_out, v_out, a_out, b_out, gate_out, *, tiles_per_seq):
    i = pl.program_id(0)
    g = g_ref[...]
    h = _rms(x_ref[...], g)
    h_before = _rms(xp_ref[...], g)[7:8, :]
    h_before = jnp.where(i % tiles_per_seq == 0, 0.0, h_before)
    rows = lax.broadcasted_iota(jnp.int32, h.shape, 0)
    shifted = jnp.where(rows == 0, h_before, pltpu.roll(h, 1, 0))
    xx = shifted - h
    mixed = [(h + xx * mix_ref[j:j + 1, :]).astype(BF16) for j in range(6)]
    xr, xw, xk, xv, xa, xg = mixed

    r = _dot(xr, wr_ref[...])
    k = _dot(xk, wk_ref[...])
    v = _dot(xv, wv_ref[...])
    z = w0_ref[...] + _dot(jnp.tanh(_dot(xw, w1_ref[...])).astype(BF16), w2_ref[...])
    log_decay = -jnp.exp(-_softplus(-z) - 0.5)
    iclr = jax.nn.sigmoid(a0_ref[...] + _dot(_dot(xa, a1_ref[...]).astype(BF16), a2_ref[...]))
    gate = _dot(jax.nn.sigmoid(_dot(xg, g1_ref[...])).astype(BF16), g2_ref[...])

    kk = k * kk_ref[...]
    kk = kk / jnp.maximum(jnp.sqrt(_head_sums(kk * kk)), 1e-12)
    k = k * (1.0 + (iclr - 1.0) * ka_ref[...])

    r_out[...] = r
    lw_out[...] = log_decay
    k_out[...] = k
    v_out[...] = v
    a_out[...] = -kk
    b_out[...] = kk * iclr
    gate_out[...] = gate


def _rwkv_pre(x, g, mix, w_rkv, w0, w1, w2, a0, a1, a2, g1, g2, k_k, k_a, seq, *, tm=256):
    n, d = x.shape
    tm = min(tm, seq)
    row = lambda i: (i, 0)
    fixed = lambda i: (0, 0)
    before = lambda i: (jnp.maximum(i * (tm // 8) - 1, 0), 0)
    vec = lambda t: t.reshape(1, d)
    pad = GATE_LORA_PADDED - g1.shape[1]
    g1p = jnp.pad(g1, ((0, 0), (0, pad))).astype(BF16)
    g2p = jnp.pad(g2, ((0, pad), (0, 0))).astype(BF16)
    lora = w1.shape[1]
    full = lambda shape: pl.BlockSpec(shape, fixed)
    out_sds = jax.ShapeDtypeStruct((n, d), F32)
    return pl.pallas_call(
        functools.partial(_rwkv_pre_kernel, tiles_per_seq=seq // tm),
        grid=(n // tm,),
        in_specs=[
            pl.BlockSpec((tm, d), row),
            pl.BlockSpec((8, d), before),
            full((1, d)), full((6, d)),
            full((d, d)), full((d, d)), full((d, d)),
            full((1, d)), full((d, lora)), full((lora, d)),
            full((1, d)), full((d, lora)), full((lora, d)),
            full((d, GATE_LORA_PADDED)), full((GATE_LORA_PADDED, d)),
            full((1, d)), full((1, d)),
        ],
        out_specs=[pl.BlockSpec((tm, d), row)] * 7,
        out_shape=[out_sds] * 7,
        compiler_params=_params(("parallel",)),
        name="rwkv_pre",
    )(x, x, vec(g), mix, w_rkv[0].astype(BF16), w_rkv[1].astype(BF16), w_rkv[2].astype(BF16),
      vec(w0), w1.astype(BF16), w2.astype(BF16), vec(a0), a1.astype(BF16), a2.astype(BF16),
      g1p, g2p, vec(k_k), vec(k_a))


def _wkv_kernel(r_ref, lw_ref, k_ref, v_ref, a_ref, b_ref, rk_ref, lng_ref, lnb_ref, z_ref, state):
    L = CHUNK
    H = RWKV_HEAD

    @pl.when(pl.program_id(1) == 0)
    def _():
        state[...] = jnp.zeros_like(state)

    def iota(shape, dim):
        return lax.broadcasted_iota(jnp.int32, shape, dim)

    first = iota((L, PAIR), 1) < H
    first_wide = (iota((L, 2 * PAIR), 1) % PAIR) < H
    t_idx = iota((L, PAIR), 0)
    s_idx = iota((L, PAIR), 1) % L
    strict = s_idx < t_idx
    t2 = iota((2 * L, PAIR), 0) % L
    s2 = iota((2 * L, PAIR), 1) % L
    incl2 = s2 <= t2
    same_head = (iota((PAIR, PAIR), 0) < H) == (iota((PAIR, PAIR), 1) < H)
    tri = (iota((L, L), 1) <= iota((L, L), 0)).astype(BF16)

    lw_all = lw_ref[...]
    p1 = lw_all.astype(BF16)
    rem = lw_all - p1.astype(F32)
    p2 = rem.astype(BF16)
    p3 = (rem - p2.astype(F32)).astype(BF16)
    cum_all = _dot(tri, p1) + _dot(tri, p2) + _dot(tri, p3)

    zeros_lp = jnp.zeros((L, PAIR), F32)

    for p in range(N_PAIRS):
        cols = slice(p * PAIR, (p + 1) * PAIR)
        r, lw, k, v = r_ref[:, cols], lw_all[:, cols], k_ref[:, cols], v_ref[:, cols]
        a, b = a_ref[:, cols], b_ref[:, cols]
        cum = cum_all[:, cols]
        total = cum[L - 1:L, :]
        w_incl = jnp.exp(cum)
        w_excl = jnp.exp(cum - lw)
        w_inv = jnp.exp(-cum)
        w_tail = jnp.exp(total - cum)
        at, rt = a * w_excl, r * w_incl
        kt, bt = k * w_inv, b * w_inv
        kh, bh = k * w_tail, b * w_tail

        bk = jnp.concatenate([bt, kt], axis=0).astype(BF16)
        kb = jnp.concatenate([kt, bt], axis=0).astype(BF16)
        at0 = jnp.where(first, at, 0.0).astype(BF16)
        at1 = jnp.where(first, 0.0, at).astype(BF16)
        res0 = _dot_nt(at0, bk)
        res1 = _dot_nt(at1, kb)
        rt01 = jnp.concatenate([jnp.where(first, rt, 0.0), jnp.where(first, 0.0, rt)], axis=0)
        top = _dot_nt(rt01.astype(BF16), kb)
        top = jnp.where(incl2, top, 0.0)
        pp = jnp.where(strict, jnp.where(first, res0, res1), 0.0)
        ak = jnp.where(strict, jnp.where(first, res1, res0), 0.0)
        v10 = jnp.concatenate([jnp.where(first, 0.0, v), jnp.where(first, v, 0.0)], axis=0)
        x = jnp.concatenate([at, _dot(ak.astype(BF16), v10.astype(BF16))], axis=1)

        steps = L.bit_length() - 1
        for it in range(steps):
            xx = jnp.concatenate([jnp.where(first_wide, x, 0.0), jnp.where(first_wide, 0.0, x)], axis=0)
            if it + 1 < steps:
                bd = jnp.concatenate([jnp.where(first, pp, 0.0), jnp.where(first, 0.0, pp)], axis=0)
                rhs = jnp.concatenate([xx, bd], axis=1)
            else:
                rhs = xx
            res = _dot(pp.astype(BF16), rhs.astype(BF16))
            x = x + res[:, :2 * PAIR]
            if it + 1 < steps:
                pp = res[:, 2 * PAIR:]
        a2, u0 = x[:, :PAIR], x[:, PAIR:]

        rhs2 = jnp.concatenate(
            [jnp.concatenate([v, zeros_lp], axis=1), jnp.concatenate([u0, a2], axis=1)], axis=0)
        khbh_t = jnp.concatenate([kh, bh], axis=0).T
        lhs = jnp.concatenate([top, khbh_t], axis=0).astype(BF16)
        fin = _dot(lhs, rhs2.astype(BF16))
        y0 = jnp.where(first, fin[:L, :PAIR], fin[L:2 * L, :PAIR])
        ra = jnp.where(first, fin[:L, PAIR:], fin[L:2 * L, PAIR:])
        c_new = jnp.where(same_head, fin[2 * L:, :PAIR], 0.0)
        ba = jnp.where(same_head, fin[2 * L:, PAIR:], 0.0)

        st = state[p]
        out = _dot(jnp.concatenate([rt + ra, ba], axis=0).astype(BF16), st.astype(BF16))
        y = out[:L] + y0
        w_total_col = jnp.exp(jnp.sum(lw.T, axis=1, keepdims=True))
        state[p] = w_total_col * st + out[L:] + c_new

        def head_sum(t):
            s0 = jnp.sum(jnp.where(first, t, 0.0), axis=-1, keepdims=True)
            s1 = jnp.sum(jnp.where(first, 0.0, t), axis=-1, keepdims=True)
            return jnp.where(first, s0, s1)

        mu = head_sum(y) * (1.0 / H)
        dev = y - mu
        var = head_sum(dev * dev) * (1.0 / H)
        yn = dev * lax.rsqrt(var + LNX_EPS) * lng_ref[:, cols] + lnb_ref[:, cols]
        bonus = head_sum(r * k * rk_ref[:, cols]) * v
        z_ref[:, cols] = yn + bonus


def _wkv(r, lw, k, v, a, b, r_k, lnx_g, lnx_b, batch, seq):
    n, d = r.shape
    nc = seq // CHUNK
    row = lambda bi, c: (bi * nc + c, 0)
    fixed = lambda bi, c: (0, 0)
    blk = pl.BlockSpec((CHUNK, d), row)
    vec = pl.BlockSpec((1, d), fixed)
    return pl.pallas_call(
        _wkv_kernel,
        grid=(batch, nc),
        in_specs=[blk] * 6 + [vec] * 3,
        out_specs=blk,
        out_shape=jax.ShapeDtypeStruct((n, d), F32),
        scratch_shapes=[pltpu.VMEM((N_PAIRS, PAIR, PAIR), F32)],
        compiler_params=_params(("arbitrary", "arbitrary")),
        name="wkv",
    )(r, lw, k, v, a, b, r_k.reshape(1, d), lnx_g.reshape(1, d), lnx_b.reshape(1, d))


def _rwkv_out_kernel(x_ref, z_ref, gate_ref, wo_ref, g_ref, o_ref):
    gated = (z_ref[...] * gate_ref[...]).astype(BF16)
    o_ref[...] = x_ref[...] + _rms(_dot(gated, wo_ref[...]), g_ref[...])


def _rwkv_out(x, z, gate, w_o, g_post, *, tm=512):
    n, d = x.shape
    tm = min(tm, n)
    row = lambda i: (i, 0)
    fixed = lambda i: (0, 0)
    return pl.pallas_call(
        _rwkv_out_kernel,
        grid=(n // tm,),
        in_specs=[pl.BlockSpec((tm, d), row)] * 3 + [pl.BlockSpec((d, d), fixed), pl.BlockSpec((1, d), fixed)],
        out_specs=pl.BlockSpec((tm, d), row),
        out_shape=jax.ShapeDtypeStruct((n, d), F32),
        compiler_params=_params(("parallel",)),
        name="rwkv_out",
    )(x, z, gate, w_o.astype(BF16), g_post.reshape(1, d))


def kernel(x, norm_g, ffn_w_in, ffn_w_out, attn_w_qkv, attn_b_qkv, attn_sinks, attn_w_o,
           rwkv_mix, rwkv_w_rkv, rwkv_w_o, rwkv_w0, rwkv_w1, rwkv_w2, rwkv_a0, rwkv_a1,
           rwkv_a2, rwkv_g1, rwkv_g2, rwkv_k_k, rwkv_k_a, rwkv_r_k, rwkv_lnx_g, rwkv_lnx_b):
    batch, seq, d = x.shape
    depth = norm_g.shape[0]
    x = x.reshape(batch * seq, d)
    ffn_tm = min(512, batch * seq)
    for i in range(depth):
        g = norm_g[i]
        j = i // 2
        x = _ffn(x, g[0], g[1], ffn_w_in[i, 0], ffn_w_out[i, 0], tm=ffn_tm)
        if i % 2 == 0:
            q, k, v = _qkv(x, g[2], attn_w_qkv[j], attn_b_qkv[j], seq)
            x = _attention(x, q, k, v, attn_sinks[j], attn_w_o[j], g[3], batch, seq)
        else:
            r, lw, k, v, a, b, gate = _rwkv_pre(
                x, g[2], rwkv_mix[j], rwkv_w_rkv[j], rwkv_w0[j], rwkv_w1[j], rwkv_w2[j],
                rwkv_a0[j], rwkv_a1[j], rwkv_a2[j], rwkv_g1[j], rwkv_g2[j],
                rwkv_k_k[j], rwkv_k_a[j], seq)
            z = _wkv(r, lw, k, v, a, b, rwkv_r_k[j], rwkv_lnx_g[j], rwkv_lnx_b[j], batch, seq)
            x = _rwkv_out(x, z, gate, rwkv_w_o[j], g[3])
        x = _ffn(x, g[4], g[5], ffn_w_in[i, 1], ffn_w_out[i, 1], tm=ffn_tm)
    return x.reshape(batch, seq, d)
```
